```python
import math
import jax
import jax.numpy as jnp
from jax import lax
import numpy as np

D_MODEL = 1024
BATCH = 32
SEQ = 256
DEPTH = 1
DEC_BATCH = 2
DEC_SEQ = 1024
PAST_LEN = 256

GRID_W = 64
ATT_WIDTH = D_MODEL // 2
HY_WIDTH = D_MODEL - ATT_WIDTH
MIX_WIDTH = ATT_WIDTH + HY_WIDTH
DA_HEADS = 4
DA_HEAD_DIM = ATT_WIDTH // (2 * DA_HEADS)
HY_ORDER = 2
HY_EMB_DIM = 33
HY_BANDS = (HY_EMB_DIM - 1) // 2
HY_FILTER_WIDTH = 64
HY_DECAY_TARGET = 1e-2
HY_SHORT_PCT = 0.3
HY_LONG_PCT = 1.5
IN_COLS = 3 * ATT_WIDTH + 3 * HY_WIDTH
D_FF = -(-8 * D_MODEL // (3 * 256)) * 256
DEEPNORM_ALPHA = (2.0 * DEPTH) ** 0.25
DEEPNORM_BETA = (8.0 * DEPTH) ** -0.25
ROPE_BASE = 10000.0
LN_EPS = 1e-5
Q_BLOCK = 128

kernel_name = 'hybrid_diffattn_hyena_prefix_dit_step'


def layer_norm(x, g=None, b=None):
    xf = x.astype(jnp.float32)
    mu = jnp.mean(xf, axis=-1, keepdims=True)
    var = jnp.mean(jnp.square(xf - mu), axis=-1, keepdims=True)
    y = (xf - mu) * lax.rsqrt(var + LN_EPS)
    if g is not None:
        y = y * g.astype(jnp.float32) + b.astype(jnp.float32)
    return y.astype(x.dtype)


def rms_norm(x, g):
    xf = x.astype(jnp.float32)
    return xf * lax.rsqrt(jnp.mean(xf * xf, axis=-1, keepdims=True) + LN_EPS) * g.astype(jnp.float32)


def adaln_params(cvec, w, b):
    m = jax.nn.silu(cvec) @ w + b
    return jnp.split(m[..., None, :], 6, axis=-1)


def axial_rope_tables(n):
    rows = n // GRID_W
    pos = jnp.arange(rows * GRID_W)
    row = (pos // GRID_W).astype(jnp.float32)
    col = (pos % GRID_W).astype(jnp.float32)
    half = DA_HEAD_DIM // 2
    inv = ROPE_BASE ** (-jnp.arange(0, half, 2, dtype=jnp.float32) / half)
    ang = jnp.stack([row[:, None] * inv, col[:, None] * inv], axis=1)
    return jnp.cos(ang), jnp.sin(ang)


def apply_axial_rope(x, cos, sin):
    xs = x.astype(jnp.float32).reshape(x.shape[:-1] + (2, 2, DA_HEAD_DIM // 4))
    x1, x2 = xs[..., 0, :], xs[..., 1, :]
    c_, s_ = cos[None, :, None], sin[None, :, None]
    out = jnp.stack([x1 * c_ - x2 * s_, x1 * s_ + x2 * c_], axis=-2)
    return out.reshape(x.shape).astype(x.dtype)


def diff_attention(q, k, v, lam):
    B, n = q.shape[:2]
    nblk = n // Q_BLOCK
    qb = q.reshape(B, nblk, Q_BLOCK, 2 * DA_HEADS, DA_HEAD_DIM).transpose(1, 0, 2, 3, 4)
    kf = k.astype(jnp.float32)
    vf = v.astype(jnp.float32)
    scale = DA_HEAD_DIM ** -0.5

    def block(qblk):
        s = jnp.einsum('bqhd,bkhd->bhqk', qblk.astype(jnp.float32), kf) * scale
        p = jax.nn.softmax(s, axis=-1).reshape(B, DA_HEADS, 2, Q_BLOCK, -1)
        a = p[:, :, 0] - lam * p[:, :, 1]
        return jnp.einsum('bhqk,bkhe->bqhe', a, vf)

    o = lax.map(block, qb)
    return o.transpose(1, 0, 2, 3, 4).reshape(B, n, DA_HEADS, 2 * DA_HEAD_DIM)


def short_conv(u, w, b):
    n = u.shape[1]
    up = jnp.pad(u, ((0, 0), (1, 1), (0, 0)))
    return up[:, :n] * w[0] + up[:, 1:n + 1] * w[1] + up[:, 2:] * w[2] + b


def hyena_filters(n, w1, b1, w2, b2, freq, w3, decay):
    f32 = jnp.float32
    pos = jnp.arange(n, dtype=f32)
    t = (pos / (n - 1))[:, None]
    bands = jnp.linspace(1e-4, HY_BANDS - 1, HY_BANDS, dtype=f32)
    ang = (2.0 * math.pi / n) * pos[:, None] * bands
    z = jnp.concatenate([t, jnp.cos(ang), -jnp.sin(ang)], axis=-1)
    fr = freq.astype(f32)
    hdn = jnp.sin(fr * (z @ w1.astype(f32) + b1.astype(f32)))
    hdn = jnp.sin(fr * (hdn @ w2.astype(f32) + b2.astype(f32)))
    filt = (hdn @ w3.astype(f32)).reshape(n, HY_ORDER, 2, HY_WIDTH)
    return filt * jnp.exp(-t[:, :, None, None] * jnp.abs(decay.astype(f32)))


def bidir_long_conv(u, h_fwd, h_bwd, skip):
    n = u.shape[1]
    circ = jnp.concatenate([h_fwd, jnp.zeros_like(h_fwd[:1]), h_bwd[:0:-1]], axis=0)
    hf = jnp.fft.rfft(circ, axis=0)
    uf = jnp.fft.rfft(u, n=2 * n, axis=1)
    y = jnp.fft.irfft(uf * hf[None], n=2 * n, axis=1)[:, :n]
    return y + u * skip


def hyena_mixer(u, conv_w, conv_b, w1, b1, w2, b2, freq, w3, decay, skip):
    f32 = jnp.float32
    n = u.shape[1]
    uc = short_conv(u.astype(f32), conv_w.astype(f32), conv_b.astype(f32))
    v, x1, x2 = jnp.split(uc, 3, axis=-1)
    filt = hyena_filters(n, w1, b1, w2, b2, freq, w3, decay)
    sk = skip.astype(f32)
    z = x1 * bidir_long_conv(v, filt[:, 0, 0], filt[:, 0, 1], sk[0])
    return x2 * bidir_long_conv(z, filt[:, 1, 0], filt[:, 1, 1], sk[1])


def setup_inputs(seed: int = 0) -> dict:
    key = jax.random.key(seed)
    ks = jax.random.split(key, 32)
    f32 = jnp.float32

    def nrm(k, shape, s):
        return s * jax.random.normal(k, shape, f32)

    decay_lo = -math.log(HY_DECAY_TARGET) / HY_LONG_PCT
    decay_hi = -math.log(HY_DECAY_TARGET) / HY_SHORT_PCT
    base_decay = jnp.broadcast_to(jnp.linspace(decay_lo, decay_hi, HY_WIDTH, dtype=f32),
                                  (DEPTH, HY_ORDER, 2, HY_WIDTH))
    return {
        'x_prompt': nrm(ks[0], (BATCH, SEQ, D_MODEL), 1.0),
        'x_sample': nrm(ks[1], (DEC_BATCH, DEC_SEQ, D_MODEL), 1.0),
        'cache_k': nrm(ks[2], (DEC_BATCH, DEPTH, PAST_LEN, 2 * DA_HEADS, DA_HEAD_DIM), 1.0),
        'cache_v': nrm(ks[3], (DEC_BATCH, DEPTH, PAST_LEN, DA_HEADS, 2 * DA_HEAD_DIM), 1.0),
        'c': nrm(ks[4], (DEC_BATCH, D_MODEL), 1.0),
        'c_ctx': nrm(ks[5], (D_MODEL,), 1.0),
        'mod_w': nrm(ks[6], (DEPTH, D_MODEL, 6 * D_MODEL), 0.5 * D_MODEL ** -0.5),
        'mod_b': nrm(ks[7], (DEPTH, 6 * D_MODEL), 0.01),
        'w_in': nrm(ks[8], (DEPTH, D_MODEL, IN_COLS), D_MODEL ** -0.5),
        'da_lq1': nrm(ks[9], (DEPTH, DA_HEAD_DIM), 0.1),
        'da_lk1': nrm(ks[10], (DEPTH, DA_HEAD_DIM), 0.1),
        'da_lq2': nrm(ks[11], (DEPTH, DA_HEAD_DIM), 0.1),
        'da_lk2': nrm(ks[12], (DEPTH, DA_HEAD_DIM), 0.1),
        'da_subln': 1.0 + nrm(ks[13], (DEPTH, 2 * DA_HEAD_DIM), 0.01),
        'hy_conv_w': nrm(ks[14], (DEPTH, 3, 3 * HY_WIDTH), 3 ** -0.5),
        'hy_conv_b': nrm(ks[15], (DEPTH, 3 * HY_WIDTH), 0.01),
        'hy_w1': nrm(ks[16], (DEPTH, HY_EMB_DIM, HY_FILTER_WIDTH), HY_EMB_DIM ** -0.5),
        'hy_b1': nrm(ks[17], (DEPTH, HY_FILTER_WIDTH), 0.01),
        'hy_w2': nrm(ks[18], (DEPTH, HY_FILTER_WIDTH, HY_FILTER_WIDTH), HY_FILTER_WIDTH ** -0.5),
        'hy_b2': nrm(ks[19], (DEPTH, HY_FILTER_WIDTH), 0.01),
        'hy_freq': 1.0 + nrm(ks[20], (DEPTH, HY_FILTER_WIDTH), 0.01),
        'hy_w3': nrm(ks[21], (DEPTH, HY_FILTER_WIDTH, HY_ORDER * 2 * HY_WIDTH), 0.1 * HY_FILTER_WIDTH ** -0.5),
        'hy_decay': base_decay + nrm(ks[22], (DEPTH, HY_ORDER, 2, HY_WIDTH), 0.1),
        'hy_skip': nrm(ks[23], (DEPTH, HY_ORDER, HY_WIDTH), 1.0),
        'w_out': nrm(ks[24], (DEPTH, MIX_WIDTH, D_MODEL), DEEPNORM_BETA * MIX_WIDTH ** -0.5),
        'ln1_g': 1.0 + nrm(ks[25], (DEPTH, D_MODEL), 0.01),
        'ln1_b': nrm(ks[26], (DEPTH, D_MODEL), 0.01),
        'w_up': nrm(ks[27], (DEPTH, D_MODEL, 2 * D_FF), D_MODEL ** -0.5),
        'w_down': nrm(ks[28], (DEPTH, D_FF, D_MODEL), DEEPNORM_BETA * D_FF ** -0.5),
        'ln2_g': 1.0 + nrm(ks[29], (DEPTH, D_MODEL), 0.01),
        'ln2_b': nrm(ks[30], (DEPTH, D_MODEL), 0.01),
    }


def reference(x_prompt, x_sample, cache_k, cache_v, c, c_ctx, mod_w, mod_b, w_in,
              da_lq1, da_lk1, da_lq2, da_lk2, da_subln, hy_conv_w, hy_conv_b,
              hy_w1, hy_b1, hy_w2, hy_b2, hy_freq, hy_w3, hy_decay, hy_skip,
              w_out, ln1_g, ln1_b, w_up, w_down, ln2_g, ln2_b):

    def layer(x, cvec, ctx_k, ctx_v, li, latent):
        B, n, _ = x.shape
        sh1, sc1, g1, sh2, sc2, g2 = adaln_params(cvec, mod_w[li], mod_b[li])
        h = layer_norm(x) * (1 + sc1) + sh1
        proj = h @ w_in[li]
        q, k, v, hy_in = jnp.split(proj, [ATT_WIDTH, 2 * ATT_WIDTH, 3 * ATT_WIDTH], axis=-1)
        q = q.reshape(B, n, 2 * DA_HEADS, DA_HEAD_DIM)
        k = k.reshape(B, n, 2 * DA_HEADS, DA_HEAD_DIM)
        v = v.reshape(B, n, DA_HEADS, 2 * DA_HEAD_DIM)
        if latent:
            cos, sin = axial_rope_tables(n)
            q = apply_axial_rope(q, cos, sin)
            keys = jnp.concatenate([ctx_k, apply_axial_rope(k, cos, sin)], axis=1)
            vals = jnp.concatenate([ctx_v, v], axis=1)
        else:
            keys, vals = k, v
        lam_init = 0.8 - 0.6 * math.exp(-0.3 * li)
        lam = (jnp.exp(jnp.sum(da_lq1[li].astype(jnp.float32) * da_lk1[li].astype(jnp.float32)))
               - jnp.exp(jnp.sum(da_lq2[li].astype(jnp.float32) * da_lk2[li].astype(jnp.float32)))
               + lam_init)
        att = diff_attention(q, keys, vals, lam)
        att = (rms_norm(att, da_subln[li]) * (1 - lam_init)).reshape(B, n, ATT_WIDTH).astype(x.dtype)
        hy = hyena_mixer(hy_in, hy_conv_w[li], hy_conv_b[li], hy_w1[li], hy_b1[li], hy_w2[li],
                         hy_b2[li], hy_freq[li], hy_w3[li], hy_decay[li], hy_skip[li]).astype(x.dtype)
        mix = jnp.concatenate([att, hy], axis=-1) @ w_out[li]
        x = layer_norm(DEEPNORM_ALPHA * x + g1 * mix, ln1_g[li], ln1_b[li])
        h = layer_norm(x) * (1 + sc2) + sh2
        gate, up = jnp.split(h @ w_up[li], 2, axis=-1)
        ffn = (jax.nn.silu(gate) * up) @ w_down[li]
        x = layer_norm(DEEPNORM_ALPHA * x + g2 * ffn, ln2_g[li], ln2_b[li])
        return x, k, v

    y_prompt = x_prompt
    ks_new, vs_new = [], []
    for li in range(DEPTH):
        y_prompt, k_l, v_l = layer(y_prompt, c_ctx, None, None, li, False)
        ks_new.append(k_l)
        vs_new.append(v_l)
    new_cache_k = jnp.stack(ks_new, axis=1)
    new_cache_v = jnp.stack(vs_new, axis=1)

    y_sample = x_sample
    for li in range(DEPTH):
        y_sample, _, _ = layer(y_sample, c, cache_k[:, li], cache_v[:, li], li, True)

    return (y_prompt, y_sample, new_cache_k, new_cache_v)
```

```python
import functools
import math

import jax
import jax.numpy as jnp
import numpy as np
from jax import lax
from jax.experimental import pallas as pl
from jax.experimental.pallas import tpu as pltpu

D_MODEL = 1024
GRID_W = 64
ATT_WIDTH = D_MODEL // 2
HY_WIDTH = D_MODEL - ATT_WIDTH
DA_HEADS = 4
DA_HEAD_DIM = ATT_WIDTH // (2 * DA_HEADS)
PAIR_WIDTH = 2 * DA_HEAD_DIM
HY_EMB_DIM = 33
HY_BANDS = (HY_EMB_DIM - 1) // 2
HY_FILTER_WIDTH = 64
IN_COLS = 3 * ATT_WIDTH + 3 * HY_WIDTH
D_FF = 2816
DEPTH = 1
DEEPNORM_ALPHA = (2.0 * DEPTH) ** 0.25
ROPE_BASE = 10000.0
LN_EPS = 1e-5
MOD_ROWS = 8

ROW_BLOCK = 256
Q_ROWS = 256
VMEM_LIMIT = 56 * 1024 * 1024

F32 = jnp.float32
BF16 = jnp.bfloat16


def _params(*sem):
    return pltpu.CompilerParams(dimension_semantics=sem, vmem_limit_bytes=VMEM_LIMIT)


def _const_spec(shape):
    nd = len(shape)
    return pl.BlockSpec(shape, lambda *_: (0,) * nd, pipeline_mode=pl.Buffered(1))


def _ln(x):
    mu = jnp.mean(x, axis=-1, keepdims=True)
    xc = x - mu
    var = jnp.mean(xc * xc, axis=-1, keepdims=True)
    return xc * lax.rsqrt(var + LN_EPS)


def _dot(a, b):
    return jnp.dot(a, b, preferred_element_type=F32)


def _rope_tables(n):
    pos = np.arange(n)
    row = (pos // GRID_W).astype(np.float64)
    col = (pos % GRID_W).astype(np.float64)
    half = DA_HEAD_DIM // 2
    inv = ROPE_BASE ** (-np.arange(0, half, 2, dtype=np.float64) / half)
    ar = row[:, None] * inv
    ac = col[:, None] * inv
    cos = np.concatenate([np.cos(ar), np.cos(ar), np.cos(ac), np.cos(ac)], axis=1)
    sin = np.concatenate([-np.sin(ar), np.sin(ar), -np.sin(ac), np.sin(ac)], axis=1)
    reps = ATT_WIDTH // DA_HEAD_DIM
    return (np.tile(cos, (1, reps)).astype(np.float32),
            np.tile(sin, (1, reps)).astype(np.float32))


def _dft_tables(n):
    k = np.arange(n)[:, None]
    t = np.arange(n)[None, :]
    ang = np.pi * ((k * t) % (2 * n)).astype(np.float64) / n
    cosm = np.cos(ang)
    sinm = np.sin(ang)
    alt = np.where(np.arange(n) % 2 == 0, 1.0, -1.0)
    fwd_re = cosm
    fwd_im = -sinm
    fwd_im[0, :] = alt
    inv_re = cosm.T / n
    inv_re[:, 0] = 1.0 / (2 * n)
    inv_im = -sinm.T / n
    inv_im[:, 0] = alt / (2 * n)
    fwd = np.concatenate([fwd_re, fwd_im], axis=0).astype(np.float32)
    inv = np.concatenate([inv_re, inv_im], axis=1).astype(np.float32)
    return fwd, inv, alt.astype(np.float32)[:, None]


def _filter_features(n):
    pos = np.arange(n, dtype=np.float64)
    t = (pos / (n - 1))[:, None]
    bands = np.linspace(1e-4, HY_BANDS - 1, HY_BANDS)
    ang = (2.0 * math.pi / n) * pos[:, None] * bands
    z = np.concatenate([t, np.cos(ang), -np.sin(ang)], axis=-1)
    zp = np.zeros((n, HY_FILTER_WIDTH), np.float64)
    zp[:, :HY_EMB_DIM] = z
    return zp.astype(np.float32), t.astype(np.float32)


def _mod_kernel(c_ref, w_ref, b_ref, o_ref):
    c = c_ref[...]
    s = c * jax.nn.sigmoid(c)
    o_ref[...] = jnp.dot(s, w_ref[...], preferred_element_type=F32,
                         precision=lax.Precision.HIGHEST) + b_ref[...]


def _modulation(cvecs, mod_w, mod_b):
    tn = 1024
    return pl.pallas_call(
        _mod_kernel,
        grid=(6 * D_MODEL // tn,),
        in_specs=[pl.BlockSpec((MOD_ROWS, D_MODEL), lambda j: (0, 0)),
                  pl.BlockSpec((D_MODEL, tn), lambda j: (0, j)),
                  pl.BlockSpec((1, tn), lambda j: (0, j))],
        out_specs=pl.BlockSpec((MOD_ROWS, tn), lambda j: (0, j)),
        out_shape=jax.ShapeDtypeStruct((MOD_ROWS, 6 * D_MODEL), F32),
        compiler_params=_params("arbitrary"),
        name="modulation",
    )(cvecs, mod_w, mod_b)


def _rope(x, cos, sin):
    lane = lax.broadcasted_iota(jnp.int32, x.shape, 1) % (DA_HEAD_DIM // 2)
    width = x.shape[1]
    quarter = DA_HEAD_DIM // 4
    partner = jnp.where(lane < quarter,
                        pltpu.roll(x, width - quarter, 1),
                        pltpu.roll(x, quarter, 1))
    return x * cos + partner * sin


def _proj_kernel(*refs, latent):
    if latent:
        x_ref, m_ref, w_ref, cos_ref, sin_ref, q_ref, k_ref, v_ref, hy_ref = refs
    else:
        x_ref, m_ref, w_ref, q_ref, k_ref, v_ref, hy_ref = refs
    m = m_ref[0]
    sh1 = m[:, 0:D_MODEL]
    sc1 = m[:, D_MODEL:2 * D_MODEL]
    h = (_ln(x_ref[...]) * (1.0 + sc1) + sh1).astype(BF16)
    q = _dot(h, w_ref[:, 0:ATT_WIDTH])
    k = _dot(h, w_ref[:, ATT_WIDTH:2 * ATT_WIDTH])
    v = _dot(h, w_ref[:, 2 * ATT_WIDTH:3 * ATT_WIDTH])
    hy_ref[...] = _dot(h, w_ref[:, 3 * ATT_WIDTH:])
    if latent:
        cos = cos_ref[...]
        sin = sin_ref[...]
        q = _rope(q, cos, sin)
        k = _rope(k, cos, sin)
    q_ref[...] = (q * (DA_HEAD_DIM ** -0.5)).astype(q_ref.dtype)
    k_ref[...] = k.astype(k_ref.dtype)
    v_ref[...] = v.astype(v_ref.dtype)


def _in_projection(x2d, m3, w_in, group_of_block, rope=None):
    tokens = x2d.shape[0]
    latent = rope is not None
    tm = ROW_BLOCK
    row = lambda i: (i, 0)
    in_specs = [pl.BlockSpec((tm, D_MODEL), row),
                pl.BlockSpec((1, 1, 6 * D_MODEL), lambda i: (group_of_block(i), 0, 0)),
                _const_spec((D_MODEL, IN_COLS))]
    args = [x2d, m3, w_in]
    if latent:
        n = rope[0].shape[0]
        blocks = n // tm
        tab = pl.BlockSpec((tm, ATT_WIDTH), lambda i: (i % blocks, 0))
        in_specs += [tab, tab]
        args += list(rope)
    kv_dtype = BF16 if latent else F32
    return pl.pallas_call(
        functools.partial(_proj_kernel, latent=latent),
        grid=(tokens // tm,),
        in_specs=in_specs,
        out_specs=[pl.BlockSpec((tm, ATT_WIDTH), row),
                   pl.BlockSpec((tm, ATT_WIDTH), row),
                   pl.BlockSpec((tm, ATT_WIDTH), row),
                   pl.BlockSpec((tm, 3 * HY_WIDTH), row)],
        out_shape=[jax.ShapeDtypeStruct((tokens, ATT_WIDTH), BF16),
                   jax.ShapeDtypeStruct((tokens, ATT_WIDTH), kv_dtype),
                   jax.ShapeDtypeStruct((tokens, ATT_WIDTH), kv_dtype),
                   jax.ShapeDtypeStruct((tokens, 3 * HY_WIDTH), F32)],
        compiler_params=_params("arbitrary"),
        name="in_projection_latent" if latent else "in_projection_context",
    )(*args)


def _attn_kernel(*refs, has_ctx, lam_init):
    if has_ctx:
        lq1, lk1, lq2, lk2, g_ref, q_ref, k_ref, v_ref, ck_ref, cv_ref, o_ref = refs
    else:
        lq1, lk1, lq2, lk2, g_ref, q_ref, k_ref, v_ref, o_ref = refs
    lam = (jnp.exp(jnp.sum(lq1[...] * lk1[...], axis=-1, keepdims=True))
           - jnp.exp(jnp.sum(lq2[...] * lk2[...], axis=-1, keepdims=True)) + lam_init)
    q = q_ref[...]
    tq = q.shape[0]
    lane = lax.broadcasted_iota(jnp.int32, q.shape, 1)
    zero = jnp.zeros_like(q)
    qq = jnp.concatenate([jnp.where(lane < DA_HEAD_DIM, q, zero),
                          jnp.where(lane >= DA_HEAD_DIM, q, zero)], axis=0)
    nt = (((1,), (1,)), ((), ()))
    keys = [k_ref[...].astype(BF16)]
    vals = [v_ref[...].astype(BF16)]
    if has_ctx:
        keys.append(ck_ref[0].astype(BF16))
        vals.append(cv_ref[0].astype(BF16))
    scores = [lax.dot_general(qq, kk, nt, preferred_element_type=F32) for kk in keys]
    mx = functools.reduce(jnp.maximum, [jnp.max(s, axis=-1, keepdims=True) for s in scores])
    es = [jnp.exp(s - mx) for s in scores]
    den = functools.reduce(jnp.add, [jnp.sum(e, axis=-1, keepdims=True) for e in es])
    rinv = 1.0 / den
    w0 = rinv[:tq]
    w1 = lam * rinv[tq:]
    o = None
    for e, vv in zip(es, vals):
        a = (e[:tq] * w0 - e[tq:] * w1).astype(BF16)
        part = _dot(a, vv)
        o = part if o is None else o + part
    ms = jnp.mean(o * o, axis=-1, keepdims=True)
    o_ref[...] = (o * lax.rsqrt(ms + LN_EPS) * g_ref[...] * (1.0 - lam_init)).astype(o_ref.dtype)


def _attention(q, k, v, lam_vecs, subln, batch, n, ctx=None):
    has_ctx = ctx is not None
    tq = Q_ROWS
    qb = n // tq
    small = pl.BlockSpec((1, DA_HEAD_DIM), lambda b, h, i: (0, 0))
    in_specs = [small, small, small, small,
                pl.BlockSpec((1, PAIR_WIDTH), lambda b, h, i: (0, 0)),
                pl.BlockSpec((tq, PAIR_WIDTH), lambda b, h, i: (b * qb + i, h)),
                pl.BlockSpec((n, PAIR_WIDTH), lambda b, h, i: (b, h)),
                pl.BlockSpec((n, PAIR_WIDTH), lambda b, h, i: (b, h))]
    args = list(lam_vecs) + [subln, q, k, v]
    if has_ctx:
        m = ctx[0].shape[1]
        cspec = pl.BlockSpec((1, m, PAIR_WIDTH), lambda b, h, i: (b, 0, h))
        in_specs += [cspec, cspec]
        args += list(ctx)
    lam_init = 0.8 - 0.6 * math.exp(-0.3 * 0)
    return pl.pallas_call(
        functools.partial(_attn_kernel, has_ctx=has_ctx, lam_init=lam_init),
        grid=(batch, DA_HEADS, qb),
        in_specs=in_specs,
        out_specs=pl.BlockSpec((tq, PAIR_WIDTH), lambda b, h, i: (b * qb + i, h)),
        out_shape=jax.ShapeDtypeStruct((batch * n, ATT_WIDTH), BF16),
        compiler_params=_params("arbitrary", "arbitrary", "arbitrary"),
        name="diff_attention_latent" if has_ctx else "diff_attention_context",
    )(*args)


def _filter_kernel(z_ref, t_ref, alt_ref, w1_ref, b1_ref, w2_ref, b2_ref, fr_ref, w3_ref,
                   dec_ref, fwd_ref, o_ref):
    n = z_ref.shape[0]
    hi = lax.Precision.HIGHEST
    fr = fr_ref[...]
    hdn = jnp.sin(fr * (jnp.dot(z_ref[...], w1_ref[...], preferred_element_type=F32,
                                precision=hi) + b1_ref[...]))
    hdn = jnp.sin(fr * (jnp.dot(hdn, w2_ref[...], preferred_element_type=F32,
                                precision=hi) + b2_ref[...]))
    filt = jnp.dot(hdn, w3_ref[...], preferred_element_type=F32, precision=hi)
    filt = filt * jnp.exp(-t_ref[...] * jnp.abs(dec_ref[...]))
    h_fwd = filt[:, :HY_WIDTH]
    row = lax.broadcasted_iota(jnp.int32, (n, HY_WIDTH), 0)
    h_bwd = jnp.where(row == 0, 0.0, filt[:, HY_WIDTH:])
    h_sum = h_fwd + h_bwd
    h_dif = h_fwd - h_bwd
    spec_re = _dot(fwd_ref[0:n, :], h_sum.astype(BF16))
    spec_im = _dot(fwd_ref[n:2 * n, :], h_dif.astype(BF16))
    nyquist = jnp.sum(h_sum * alt_ref[...], axis=0, keepdims=True)
    o_ref[0] = spec_re
    o_ref[1] = jnp.where(row == 0, 0.0, spec_im)
    o_ref[2] = jnp.where(row == 0, nyquist, spec_re)


def _filter_spectra(n, z, t, alt, fwd, w1p, b1, w2, b2, fr, w3, dec):
    fw = HY_FILTER_WIDTH
    cols = 2 * HY_WIDTH
    c2 = lambda o: (0, 0)
    return pl.pallas_call(
        _filter_kernel,
        grid=(2,),
        in_specs=[pl.BlockSpec((n, fw), c2), pl.BlockSpec((n, 1), c2), pl.BlockSpec((n, 1), c2),
                  pl.BlockSpec((fw, fw), c2), pl.BlockSpec((1, fw), c2),
                  pl.BlockSpec((fw, fw), c2), pl.BlockSpec((1, fw), c2),
                  pl.BlockSpec((1, fw), c2),
                  pl.BlockSpec((fw, cols), lambda o: (0, o)),
                  pl.BlockSpec((1, cols), lambda o: (0, o)),
                  _const_spec((2 * n, n))],
        out_specs=pl.BlockSpec((3, n, HY_WIDTH), lambda o: (o, 0, 0)),
        out_shape=jax.ShapeDtypeStruct((6, n, HY_WIDTH), F32),
        compiler_params=_params("arbitrary"),
        name=f"hyena_filter_spectra_{n}",
    )(z, t, alt, w1p, b1, w2, b2, fr, w3, dec, fwd)


def _hyena_kernel(u_ref, cw_ref, cb_ref, spec_ref, fwd_ref, inv_ref, skip_ref, o_ref):
    n = u_ref.shape[0]
    u = u_ref[...]
    row = lax.broadcasted_iota(jnp.int32, u.shape, 0)
    u_prev = jnp.where(row == 0, 0.0, pltpu.roll(u, 1, 0))
    u_next = jnp.where(row == n - 1, 0.0, pltpu.roll(u, n - 1, 0))
    uc = u_prev * cw_ref[0:1, :] + u * cw_ref[1:2, :] + u_next * cw_ref[2:3, :] + cb_ref[...]
    v = uc[:, 0:HY_WIDTH]
    x1 = uc[:, HY_WIDTH:2 * HY_WIDTH]
    x2 = uc[:, 2 * HY_WIDTH:]

    def long_conv(sig, order):
        sb = sig.astype(BF16)
        u_re = _dot(fwd_ref[0:n, :], sb)
        u_im = _dot(fwd_ref[n:2 * n, :], sb)
        a = spec_ref[3 * order]
        b = spec_ref[3 * order + 1]
        d = spec_ref[3 * order + 2]
        y_re = (u_re * a - u_im * b).astype(BF16)
        y_im = (u_re * b + u_im * d).astype(BF16)
        y = _dot(inv_ref[:, 0:n], y_re) + _dot(inv_ref[:, n:2 * n], y_im)
        return y + sig * skip_ref[order:order + 1, :]

    z = x1 * long_conv(v, 0)
    o_ref[...] = (x2 * long_conv(z, 1)).astype(o_ref.dtype)


def _hyena(hy_in, conv_w, conv_b, spec, fwd, inv, skip, batch, n):
    c2 = lambda b: (0, 0)
    return pl.pallas_call(
        _hyena_kernel,
        grid=(batch,),
        in_specs=[pl.BlockSpec((n, 3 * HY_WIDTH), lambda b: (b, 0)),
                  pl.BlockSpec((3, 3 * HY_WIDTH), c2),
                  pl.BlockSpec((1, 3 * HY_WIDTH), c2),
                  _const_spec((6, n, HY_WIDTH)),
                  _const_spec((2 * n, n)),
                  _const_spec((n, 2 * n)),
                  pl.BlockSpec((2, HY_WIDTH), c2)],
        out_specs=pl.BlockSpec((n, HY_WIDTH), lambda b: (b, 0)),
        out_shape=jax.ShapeDtypeStruct((batch * n, HY_WIDTH), BF16),
        compiler_params=_params("arbitrary"),
        name=f"hyena_mixer_{n}",
    )(hy_in, conv_w, conv_b, spec, fwd, inv, skip)


def _ffn_kernel(x_ref, att_ref, hy_ref, m_ref, woa_ref, woh_ref, wg_ref, wu_ref, wd_ref,
                g1_ref, b1_ref, g2_ref, b2_ref, o_ref):
    m = m_ref[0]
    gate1 = m[:, 2 * D_MODEL:3 * D_MODEL]
    sh2 = m[:, 3 * D_MODEL:4 * D_MODEL]
    sc2 = m[:, 4 * D_MODEL:5 * D_MODEL]
    gate2 = m[:, 5 * D_MODEL:6 * D_MODEL]
    mix = _dot(att_ref[...], woa_ref[...]) + _dot(hy_ref[...], woh_ref[...])
    x1 = _ln(DEEPNORM_ALPHA * x_ref[...] + gate1 * mix) * g1_ref[...] + b1_ref[...]
    h = (_ln(x1) * (1.0 + sc2) + sh2).astype(BF16)
    g = _dot(h, wg_ref[...])
    up = _dot(h, wu_ref[...])
    act = (g * jax.nn.sigmoid(g) * up).astype(BF16)
    ffn = _dot(act, wd_ref[...])
    o_ref[...] = _ln(DEEPNORM_ALPHA * x1 + gate2 * ffn) * g2_ref[...] + b2_ref[...]


def _out_ffn(x2d, att, hy, m3, group_of_block, woa, woh, wg, wu, wd, ln1g, ln1b, ln2g, ln2b):
    tokens = x2d.shape[0]
    tm = ROW_BLOCK
    row = lambda i: (i, 0)
    vec = pl.BlockSpec((1, D_MODEL), lambda i: (0, 0))
    return pl.pallas_call(
        _ffn_kernel,
        grid=(tokens // tm,),
        in_specs=[pl.BlockSpec((tm, D_MODEL), row),
                  pl.BlockSpec((tm, ATT_WIDTH), row),
                  pl.BlockSpec((tm, HY_WIDTH), row),
                  pl.BlockSpec((1, 1, 6 * D_MODEL), lambda i: (group_of_block(i), 0, 0)),
                  _const_spec((ATT_WIDTH, D_MODEL)),
                  _const_spec((HY_WIDTH, D_MODEL)),
                  _const_spec((D_MODEL, D_FF)),
                  _const_spec((D_MODEL, D_FF)),
                  _const_spec((D_FF, D_MODEL)),
                  vec, vec, vec, vec],
        out_specs=pl.BlockSpec((tm, D_MODEL), row),
        out_shape=jax.ShapeDtypeStruct((tokens, D_MODEL), F32),
        compiler_params=_params("arbitrary"),
        name="out_proj_ffn",
    )(x2d, att, hy, m3, woa, woh, wg, wu, wd, ln1g, ln1b, ln2g, ln2b)


def kernel(x_prompt, x_sample, cache_k, cache_v, c, c_ctx, mod_w, mod_b, w_in, da_lq1, da_lk1, da_lq2, da_lk2, da_subln, hy_conv_w, hy_conv_b, hy_w1, hy_b1, hy_w2, hy_b2, hy_freq, hy_w3, hy_decay, hy_skip, w_out, ln1_g, ln1_b, w_up, w_down, ln2_g, ln2_b):
    li = 0
    batch, seq, _ = x_prompt.shape
    dec_batch, dec_seq, _ = x_sample.shape
    past = cache_k.shape[2]

    cvecs = jnp.concatenate(
        [c_ctx[None, :], c, jnp.zeros((MOD_ROWS - 1 - dec_batch, D_MODEL), F32)], axis=0)
    m3 = _modulation(cvecs, mod_w[li], mod_b[li][None, :]).reshape(MOD_ROWS, 1, 6 * D_MODEL)

    w_in_b = w_in[li].astype(BF16)
    woa = w_out[li, :ATT_WIDTH].astype(BF16)
    woh = w_out[li, ATT_WIDTH:].astype(BF16)
    wg = w_up[li, :, :D_FF].astype(BF16)
    wu = w_up[li, :, D_FF:].astype(BF16)
    wd = w_down[li].astype(BF16)
    lam_vecs = [a[li][None, :] for a in (da_lq1, da_lk1, da_lq2, da_lk2)]
    subln = da_subln[li][None, :]
    ln = [a[li][None, :] for a in (ln1_g, ln1_b, ln2_g, ln2_b)]

    w1p = jnp.zeros((HY_FILTER_WIDTH, HY_FILTER_WIDTH), F32).at[:HY_EMB_DIM].set(hy_w1[li])
    filt_args = (w1p, hy_b1[li][None, :], hy_w2[li], hy_b2[li][None, :], hy_freq[li][None, :],
                 hy_w3[li], hy_decay[li].reshape(1, -1))
    conv_w = hy_conv_w[li]
    conv_b = hy_conv_b[li][None, :]
    skip = hy_skip[li]

    def mixer_and_ffn(x2d, q, k, v, hy_in, group_of_block, nb, n, ctx):
        att = _attention(q, k, v, lam_vecs, subln, nb, n, ctx)
        fwd32, inv32, alt = _dft_tables(n)
        z, t = _filter_features(n)
        fwd = jnp.asarray(fwd32).astype(BF16)
        inv = jnp.asarray(inv32).astype(BF16)
        spec = _filter_spectra(n, jnp.asarray(z), jnp.asarray(t), jnp.asarray(alt), fwd, *filt_args)
        hy = _hyena(hy_in, conv_w, conv_b, spec, fwd, inv, skip, nb, n)
        return _out_ffn(x2d, att, hy, m3, group_of_block, woa, woh, wg, wu, wd, *ln)

    xp = x_prompt.reshape(batch * seq, D_MODEL)
    ctx_group = lambda i: 0
    q, k, v, hy_in = _in_projection(xp, m3, w_in_b, ctx_group)
    y_prompt = mixer_and_ffn(xp, q, k, v, hy_in, ctx_group, batch, seq, None)
    new_cache_k = k.reshape(batch, 1, seq, 2 * DA_HEADS, DA_HEAD_DIM)
    new_cache_v = v.reshape(batch, 1, seq, DA_HEADS, 2 * DA_HEAD_DIM)

    xs = x_sample.reshape(dec_batch * dec_seq, D_MODEL)
    blocks_per_seq = dec_seq // ROW_BLOCK
    lat_group = lambda i: 1 + i // blocks_per_seq
    rope = tuple(jnp.asarray(a) for a in _rope_tables(dec_seq))
    q, k, v, hy_in = _in_projection(xs, m3, w_in_b, lat_group, rope)
    ctx = (cache_k[:, li].reshape(dec_batch, past, ATT_WIDTH),
           cache_v[:, li].reshape(dec_batch, past, ATT_WIDTH))
    y_sample = mixer_and_ffn(xs, q, k, v, hy_in, lat_group, dec_batch, dec_seq, ctx)

    return (y_prompt.reshape(batch, seq, D_MODEL),
            y_sample.reshape(dec_batch, dec_seq, D_MODEL),
            new_cache_k, new_cache_v)
```

```python
import functools
import math

import jax
import jax.numpy as jnp
import numpy as np
from jax import lax
from jax.experimental import pallas as pl
from jax.experimental.pallas import tpu as pltpu

D_MODEL = 1024
GRID_W = 64
ATT_WIDTH = D_MODEL // 2
HY_WIDTH = D_MODEL - ATT_WIDTH
DA_HEADS = 4
DA_HEAD_DIM = ATT_WIDTH // (2 * DA_HEADS)
PAIR_WIDTH = 2 * DA_HEAD_DIM
HY_EMB_DIM = 33
HY_BANDS = (HY_EMB_DIM - 1) // 2
HY_FILTER_WIDTH = 64
IN_COLS = 3 * ATT_WIDTH + 3 * HY_WIDTH
D_FF = 2816
DEPTH = 1
DEEPNORM_ALPHA = (2.0 * DEPTH) ** 0.25
ROPE_BASE = 10000.0
LN_EPS = 1e-5
SUBLANES = 8
MOD_ROWS = SUBLANES

ROW_BLOCK = 256
Q_ROWS = 256
HYENA_ROWS_PER_STEP = 512
VMEM_LIMIT = 56 * 1024 * 1024

F32 = jnp.float32
BF16 = jnp.bfloat16


def _params(*sem):
    return pltpu.CompilerParams(dimension_semantics=sem, vmem_limit_bytes=VMEM_LIMIT)


def _const_spec(shape):
    nd = len(shape)
    return pl.BlockSpec(shape, lambda *_: (0,) * nd, pipeline_mode=pl.Buffered(1))


def _ln(x):
    mu = jnp.mean(x, axis=-1, keepdims=True)
    xc = x - mu
    var = jnp.mean(xc * xc, axis=-1, keepdims=True)
    return xc * lax.rsqrt(var + LN_EPS)


def _dot(a, b):
    return jnp.dot(a, b, preferred_element_type=F32)


def _split(x):
    hi = x.astype(BF16)
    return hi, (x - hi.astype(F32)).astype(BF16)


def _dot3(a, b):
    a_hi, a_lo = _split(a)
    b_hi, b_lo = _split(b)
    return _dot(a_hi, b_hi) + (_dot(a_lo, b_hi) + _dot(a_hi, b_lo))


def _rope_tables(n):
    pos = np.arange(n)
    row = (pos // GRID_W).astype(np.float64)
    col = (pos % GRID_W).astype(np.float64)
    half = DA_HEAD_DIM // 2
    inv = ROPE_BASE ** (-np.arange(0, half, 2, dtype=np.float64) / half)
    ar = row[:, None] * inv
    ac = col[:, None] * inv
    cos = np.concatenate([np.cos(ar), np.cos(ar), np.cos(ac), np.cos(ac)], axis=1)
    sin = np.concatenate([-np.sin(ar), np.sin(ar), -np.sin(ac), np.sin(ac)], axis=1)
    reps = ATT_WIDTH // DA_HEAD_DIM
    return (np.tile(cos, (1, reps)).astype(np.float32),
            np.tile(sin, (1, reps)).astype(np.float32))


def _dft_tables(n):
    k = np.arange(n)[:, None]
    t = np.arange(n)[None, :]
    ang = np.pi * ((k * t) % (2 * n)).astype(np.float64) / n
    cosm = np.cos(ang)
    sinm = np.sin(ang)
    alt = np.where(np.arange(n) % 2 == 0, 1.0, -1.0)
    fwd_re = cosm
    fwd_im = -sinm
    fwd_im[0, :] = alt
    inv_re = cosm.T / n
    inv_re[:, 0] = 1.0 / (2 * n)
    inv_im = -sinm.T / n
    inv_im[:, 0] = alt / (2 * n)
    fwd = np.concatenate([fwd_re, fwd_im], axis=0).astype(np.float32)
    inv = np.concatenate([inv_re, inv_im], axis=1).astype(np.float32)
    return fwd, inv, alt.astype(np.float32)[:, None]


def _filter_features(n):
    pos = np.arange(n, dtype=np.float64)
    t = (pos / (n - 1))[:, None]
    bands = np.linspace(1e-4, HY_BANDS - 1, HY_BANDS)
    ang = (2.0 * math.pi / n) * pos[:, None] * bands
    z = np.concatenate([t, np.cos(ang), -np.sin(ang)], axis=-1)
    zp = np.zeros((n, HY_FILTER_WIDTH), np.float64)
    zp[:, :HY_EMB_DIM] = z
    return zp.astype(np.float32), t.astype(np.float32)


def _mod_kernel(c_ref, w_ref, b_ref, o_ref):
    c = c_ref[...]
    s = c * jax.nn.sigmoid(c)
    o_ref[...] = _dot3(s, w_ref[...]) + b_ref[...]


def _modulation(cvecs, mod_w, mod_b):
    tn = 1024
    return pl.pallas_call(
        _mod_kernel,
        grid=(6 * D_MODEL // tn,),
        in_specs=[pl.BlockSpec((MOD_ROWS, D_MODEL), lambda j: (0, 0)),
                  pl.BlockSpec((D_MODEL, tn), lambda j: (0, j)),
                  pl.BlockSpec((1, tn), lambda j: (0, j))],
        out_specs=pl.BlockSpec((MOD_ROWS, tn), lambda j: (0, j)),
        out_shape=jax.ShapeDtypeStruct((MOD_ROWS, 6 * D_MODEL), F32),
        compiler_params=_params("arbitrary"),
        name="modulation",
    )(cvecs, mod_w, mod_b)


def _rope(x, cos, sin):
    lane = lax.broadcasted_iota(jnp.int32, x.shape, 1) % (DA_HEAD_DIM // 2)
    width = x.shape[1]
    quarter = DA_HEAD_DIM // 4
    partner = jnp.where(lane < quarter,
                        pltpu.roll(x, width - quarter, 1),
                        pltpu.roll(x, quarter, 1))
    return x * cos + partner * sin


def _proj_kernel(*refs, latent):
    if latent:
        x_ref, m_ref, w_ref, cos_ref, sin_ref, q_ref, k_ref, v_ref, hy_ref = refs
    else:
        x_ref, m_ref, w_ref, q_ref, k_ref, v_ref, hy_ref = refs
    m = m_ref[0]
    sh1 = m[:, 0:D_MODEL]
    sc1 = m[:, D_MODEL:2 * D_MODEL]
    h = (_ln(x_ref[...]) * (1.0 + sc1) + sh1).astype(BF16)
    q = _dot(h, w_ref[:, 0:ATT_WIDTH])
    k = _dot(h, w_ref[:, ATT_WIDTH:2 * ATT_WIDTH])
    v = _dot(h, w_ref[:, 2 * ATT_WIDTH:3 * ATT_WIDTH])
    hy_ref[...] = _dot(h, w_ref[:, 3 * ATT_WIDTH:])
    if latent:
        cos = cos_ref[...]
        sin = sin_ref[...]
        q = _rope(q, cos, sin)
        k = _rope(k, cos, sin)
    q_ref[...] = (q * (DA_HEAD_DIM ** -0.5)).astype(q_ref.dtype)
    k_ref[...] = k.astype(k_ref.dtype)
    v_ref[...] = v.astype(v_ref.dtype)


def _in_projection(x2d, m3, w_in, group_of_block, rope=None):
    tokens = x2d.shape[0]
    latent = rope is not None
    tm = ROW_BLOCK
    row = lambda i: (i, 0)
    in_specs = [pl.BlockSpec((tm, D_MODEL), row),
                pl.BlockSpec((1, 1, 6 * D_MODEL), lambda i: (group_of_block(i), 0, 0)),
                _const_spec((D_MODEL, IN_COLS))]
    args = [x2d, m3, w_in]
    if latent:
        n = rope[0].shape[0]
        blocks = n // tm
        tab = pl.BlockSpec((tm, ATT_WIDTH), lambda i: (i % blocks, 0))
        in_specs += [tab, tab]
        args += list(rope)
    kv_dtype = BF16 if latent else F32
    return pl.pallas_call(
        functools.partial(_proj_kernel, latent=latent),
        grid=(tokens // tm,),
        in_specs=in_specs,
        out_specs=[pl.BlockSpec((tm, ATT_WIDTH), row),
                   pl.BlockSpec((tm, ATT_WIDTH), row),
                   pl.BlockSpec((tm, ATT_WIDTH), row),
                   pl.BlockSpec((tm, 3 * HY_WIDTH), row)],
        out_shape=[jax.ShapeDtypeStruct((tokens, ATT_WIDTH), BF16),
                   jax.ShapeDtypeStruct((tokens, ATT_WIDTH), kv_dtype),
                   jax.ShapeDtypeStruct((tokens, ATT_WIDTH), kv_dtype),
                   jax.ShapeDtypeStruct((tokens, 3 * HY_WIDTH), F32)],
        compiler_params=_params("arbitrary"),
        name="in_projection_latent" if latent else "in_projection_context",
    )(*args)


def _attn_kernel(*refs, has_ctx, lam_init):
    if has_ctx:
        lq1, lk1, lq2, lk2, g_ref, q_ref, k_ref, v_ref, ck_ref, cv_ref, o_ref = refs
    else:
        lq1, lk1, lq2, lk2, g_ref, q_ref, k_ref, v_ref, o_ref = refs
    lam = (jnp.exp(jnp.sum(lq1[...] * lk1[...], axis=-1, keepdims=True))
           - jnp.exp(jnp.sum(lq2[...] * lk2[...], axis=-1, keepdims=True)) + lam_init)
    tq = q_ref.shape[0]
    lane = lax.broadcasted_iota(jnp.int32, (tq, PAIR_WIDTH), 1)
    nt = (((1,), (1,)), ((), ()))
    for h in range(DA_HEADS):
        cols = slice(h * PAIR_WIDTH, (h + 1) * PAIR_WIDTH)
        q = q_ref[:, cols]
        zero = jnp.zeros_like(q)
        qq = jnp.concatenate([jnp.where(lane < DA_HEAD_DIM, q, zero),
                              jnp.where(lane >= DA_HEAD_DIM, q, zero)], axis=0)
        keys = [k_ref[:, cols].astype(BF16)]
        vals = [v_ref[:, cols].astype(BF16)]
        if has_ctx:
            keys.append(ck_ref[0, :, cols].astype(BF16))
            vals.append(cv_ref[0, :, cols].astype(BF16))
        scores = [lax.dot_general(qq, kk, nt, preferred_element_type=F32) for kk in keys]
        mx = functools.reduce(jnp.maximum,
                              [jnp.max(s, axis=-1, keepdims=True) for s in scores])
        acc = None
        for s, vv in zip(scores, vals):
            e = jnp.exp(s - mx).astype(BF16)
            v_ext = jnp.concatenate([vv, jnp.ones_like(vv)], axis=1)
            part = _dot(e, v_ext)
            acc = part if acc is None else acc + part
        p = acc[:, :PAIR_WIDTH] * (1.0 / acc[:, PAIR_WIDTH:])
        o = p[:tq] - lam * p[tq:]
        ms = jnp.mean(o * o, axis=-1, keepdims=True)
        o_ref[:, cols] = (o * lax.rsqrt(ms + LN_EPS) * g_ref[...]
                          * (1.0 - lam_init)).astype(o_ref.dtype)


def _attention(q, k, v, lam_vecs, subln, batch, n, ctx=None):
    has_ctx = ctx is not None
    tq = Q_ROWS
    qb = n // tq
    small = pl.BlockSpec((1, DA_HEAD_DIM), lambda b, i: (0, 0))
    in_specs = [small, small, small, small,
                pl.BlockSpec((1, PAIR_WIDTH), lambda b, i: (0, 0)),
                pl.BlockSpec((tq, ATT_WIDTH), lambda b, i: (b * qb + i, 0)),
                pl.BlockSpec((n, ATT_WIDTH), lambda b, i: (b, 0)),
                pl.BlockSpec((n, ATT_WIDTH), lambda b, i: (b, 0))]
    args = list(lam_vecs) + [subln, q, k, v]
    if has_ctx:
        m = ctx[0].shape[1]
        cspec = pl.BlockSpec((1, m, ATT_WIDTH), lambda b, i: (b, 0, 0))
        in_specs += [cspec, cspec]
        args += list(ctx)
    lam_init = 0.8 - 0.6 * math.exp(-0.3 * 0)
    return pl.pallas_call(
        functools.partial(_attn_kernel, has_ctx=has_ctx, lam_init=lam_init),
        grid=(batch, qb),
        in_specs=in_specs,
        out_specs=pl.BlockSpec((tq, ATT_WIDTH), lambda b, i: (b * qb + i, 0)),
        out_shape=jax.ShapeDtypeStruct((batch * n, ATT_WIDTH), BF16),
        compiler_params=_params("arbitrary", "arbitrary"),
        name="diff_attention_latent" if has_ctx else "diff_attention_context",
    )(*args)


def _filter_kernel(z_ref, t_ref, alt_ref, w1_ref, b1_ref, w2_ref, b2_ref, fr_ref, w3_ref,
                   dec_ref, fwd_ref, o_ref, hdn_ref):
    n = z_ref.shape[0]

    @pl.when(pl.program_id(0) == 0)
    def _():
        fr = fr_ref[...]
        hdn = jnp.sin(fr * (_dot3(z_ref[...], w1_ref[...]) + b1_ref[...]))
        hdn_ref[...] = jnp.sin(fr * (_dot3(hdn, w2_ref[...]) + b2_ref[...]))

    filt = _dot3(hdn_ref[...], w3_ref[...])
    filt = filt * jnp.exp(-t_ref[...] * jnp.abs(dec_ref[...]))
    h_fwd = filt[:, :HY_WIDTH]
    row = lax.broadcasted_iota(jnp.int32, (n, HY_WIDTH), 0)
    h_bwd = jnp.where(row == 0, 0.0, filt[:, HY_WIDTH:])
    h_sum = h_fwd + h_bwd
    h_dif = h_fwd - h_bwd
    spec_re = _dot(fwd_ref[0:n, :], h_sum.astype(BF16))
    spec_im = _dot(fwd_ref[n:2 * n, :], h_dif.astype(BF16))
    nyquist = jnp.sum(h_sum * alt_ref[...], axis=0, keepdims=True)
    o_ref[0] = spec_re
    o_ref[1] = jnp.where(row == 0, 0.0, spec_im)
    o_ref[2] = jnp.where(row == 0, nyquist, spec_re)


def _filter_spectra(n, z, t, alt, fwd, w1p, b1, w2, b2, fr, w3, dec):
    fw = HY_FILTER_WIDTH
    cols = 2 * HY_WIDTH
    c2 = lambda o: (0, 0)
    return pl.pallas_call(
        _filter_kernel,
        grid=(2,),
        in_specs=[pl.BlockSpec((n, fw), c2), pl.BlockSpec((n, 1), c2), pl.BlockSpec((n, 1), c2),
                  pl.BlockSpec((fw, fw), c2), pl.BlockSpec((1, fw), c2),
                  pl.BlockSpec((fw, fw), c2), pl.BlockSpec((1, fw), c2),
                  pl.BlockSpec((1, fw), c2),
                  pl.BlockSpec((fw, cols), lambda o: (0, o)),
                  pl.BlockSpec((1, cols), lambda o: (0, o)),
                  _const_spec((2 * n, n))],
        out_specs=pl.BlockSpec((3, n, HY_WIDTH), lambda o: (o, 0, 0)),
        out_shape=jax.ShapeDtypeStruct((6, n, HY_WIDTH), F32),
        scratch_shapes=[pltpu.VMEM((n, fw), F32)],
        compiler_params=_params("arbitrary"),
        name=f"hyena_filter_spectra_{n}",
    )(z, t, alt, w1p, b1, w2, b2, fr, w3, dec, fwd)


def _hyena_kernel(u_ref, cw_ref, cb_ref, spec_ref, fwd_ref, inv_ref, skip_ref, o_ref, *, seqs):
    n = u_ref.shape[0] // seqs

    def long_conv(sig, order):
        sb = sig.astype(BF16)
        u_re = _dot(fwd_ref[0:n, :], sb)
        u_im = _dot(fwd_ref[n:2 * n, :], sb)
        a = spec_ref[3 * order]
        b = spec_ref[3 * order + 1]
        d = spec_ref[3 * order + 2]
        y_re = (u_re * a - u_im * b).astype(BF16)
        y_im = (u_re * b + u_im * d).astype(BF16)
        y = _dot(inv_ref[:, 0:n], y_re) + _dot(inv_ref[:, n:2 * n], y_im)
        return y + sig * skip_ref[order:order + 1, :]

    for s in range(seqs):
        rows = slice(s * n, (s + 1) * n)
        u = u_ref[rows, :]
        row = lax.broadcasted_iota(jnp.int32, u.shape, 0)
        u_prev = jnp.where(row == 0, 0.0, pltpu.roll(u, 1, 0))
        u_next = jnp.where(row == n - 1, 0.0, pltpu.roll(u, n - 1, 0))
        uc = (u_prev * cw_ref[0:1, :] + u * cw_ref[1:2, :] + u_next * cw_ref[2:3, :]
              + cb_ref[...])
        v = uc[:, 0:HY_WIDTH]
        x1 = uc[:, HY_WIDTH:2 * HY_WIDTH]
        x2 = uc[:, 2 * HY_WIDTH:]
        z = x1 * long_conv(v, 0)
        o_ref[rows, :] = (x2 * long_conv(z, 1)).astype(o_ref.dtype)


def _hyena(hy_in, conv_w, conv_b, spec, fwd, inv, skip, batch, n):
    c2 = lambda b: (0, 0)
    seqs = max(1, min(batch, HYENA_ROWS_PER_STEP // n))
    in_specs = [pl.BlockSpec((seqs * n, 3 * HY_WIDTH), lambda b: (b, 0)),
                pl.BlockSpec((3, 3 * HY_WIDTH), c2),
                pl.BlockSpec((1, 3 * HY_WIDTH), c2),
                _const_spec((6, n, HY_WIDTH)),
                _const_spec((2 * n, n)),
                _const_spec((n, 2 * n)),
                pl.BlockSpec((2, HY_WIDTH), c2)]
    args = [hy_in, conv_w, conv_b, spec, fwd, inv, skip]
    return pl.pallas_call(
        functools.partial(_hyena_kernel, seqs=seqs),
        grid=(batch // seqs,),
        in_specs=in_specs,
        out_specs=pl.BlockSpec((seqs * n, HY_WIDTH), lambda b: (b, 0)),
        out_shape=jax.ShapeDtypeStruct((batch * n, HY_WIDTH), BF16),
        compiler_params=_params("arbitrary"),
        name=f"hyena_mixer_{n}",
    )(*args)


def _ffn_kernel(x_ref, att_ref, hy_ref, m_ref, woa_ref, woh_ref, wg_ref, wu_ref, wd_ref,
                g1_ref, b1_ref, g2_ref, b2_ref, o_ref):
    m = m_ref[0]
    gate1 = m[:, 2 * D_MODEL:3 * D_MODEL]
    sh2 = m[:, 3 * D_MODEL:4 * D_MODEL]
    sc2 = m[:, 4 * D_MODEL:5 * D_MODEL]
    gate2 = m[:, 5 * D_MODEL:6 * D_MODEL]
    mix = _dot(att_ref[...], woa_ref[...]) + _dot(hy_ref[...], woh_ref[...])
    x1 = _ln(DEEPNORM_ALPHA * x_ref[...] + gate1 * mix) * g1_ref[...] + b1_ref[...]
    h = (_ln(x1) * (1.0 + sc2) + sh2).astype(BF16)
    g = _dot(h, wg_ref[...])
    up = _dot(h, wu_ref[...])
    act = (g * jax.nn.sigmoid(g) * up).astype(BF16)
    ffn = _dot(act, wd_ref[...])
    o_ref[...] = _ln(DEEPNORM_ALPHA * x1 + gate2 * ffn) * g2_ref[...] + b2_ref[...]


def _out_ffn(x2d, att, hy, m3, group_of_block, woa, woh, wg, wu, wd, ln1g, ln1b, ln2g, ln2b):
    tokens = x2d.shape[0]
    tm = ROW_BLOCK
    row = lambda i: (i, 0)
    vec = pl.BlockSpec((1, D_MODEL), lambda i: (0, 0))
    return pl.pallas_call(
        _ffn_kernel,
        grid=(tokens // tm,),
        in_specs=[pl.BlockSpec((tm, D_MODEL), row),
                  pl.BlockSpec((tm, ATT_WIDTH), row),
                  pl.BlockSpec((tm, HY_WIDTH), row),
                  pl.BlockSpec((1, 1, 6 * D_MODEL), lambda i: (group_of_block(i), 0, 0)),
                  _const_spec((ATT_WIDTH, D_MODEL)),
                  _const_spec((HY_WIDTH, D_MODEL)),
                  _const_spec((D_MODEL, D_FF)),
                  _const_spec((D_MODEL, D_FF)),
                  _const_spec((D_FF, D_MODEL)),
                  vec, vec, vec, vec],
        out_specs=pl.BlockSpec((tm, D_MODEL), row),
        out_shape=jax.ShapeDtypeStruct((tokens, D_MODEL), F32),
        compiler_params=_params("arbitrary"),
        name="out_proj_ffn",
    )(x2d, att, hy, m3, woa, woh, wg, wu, wd, ln1g, ln1b, ln2g, ln2b)


def kernel(x_prompt, x_sample, cache_k, cache_v, c, c_ctx, mod_w, mod_b, w_in, da_lq1, da_lk1, da_lq2, da_lk2, da_subln, hy_conv_w, hy_conv_b, hy_w1, hy_b1, hy_w2, hy_b2, hy_freq, hy_w3, hy_decay, hy_skip, w_out, ln1_g, ln1_b, w_up, w_down, ln2_g, ln2_b):
    li = 0
    batch, seq, _ = x_prompt.shape
    dec_batch, dec_seq, _ = x_sample.shape
    past = cache_k.shape[2]

    cvecs = jnp.concatenate(
        [c_ctx[None, :], c, jnp.zeros((MOD_ROWS - 1 - dec_batch, D_MODEL), F32)], axis=0)
    m3 = _modulation(cvecs, mod_w[li], mod_b[li][None, :]).reshape(MOD_ROWS, 1, 6 * D_MODEL)

    w_in_b = w_in[li].astype(BF16)
    woa = w_out[li, :ATT_WIDTH].astype(BF16)
    woh = w_out[li, ATT_WIDTH:].astype(BF16)
    wg = w_up[li, :, :D_FF].astype(BF16)
    wu = w_up[li, :, D_FF:].astype(BF16)
    wd = w_down[li].astype(BF16)
    lam_vecs = [a[li][None, :] for a in (da_lq1, da_lk1, da_lq2, da_lk2)]
    subln = da_subln[li][None, :]
    ln = [a[li][None, :] for a in (ln1_g, ln1_b, ln2_g, ln2_b)]

    w1p = jnp.zeros((HY_FILTER_WIDTH, HY_FILTER_WIDTH), F32).at[:HY_EMB_DIM].set(hy_w1[li])
    filt_args = (w1p, hy_b1[li][None, :], hy_w2[li], hy_b2[li][None, :], hy_freq[li][None, :],
                 hy_w3[li], hy_decay[li].reshape(1, -1))
    conv_w = hy_conv_w[li]
    conv_b = hy_conv_b[li][None, :]
    skip = hy_skip[li]

    def mixer_and_ffn(x2d, q, k, v, hy_in, group_of_block, nb, n, ctx):
        att = _attention(q, k, v, lam_vecs, subln, nb, n, ctx)
        fwd32, inv32, alt = _dft_tables(n)
        z, t = _filter_features(n)
        fwd = jnp.asarray(fwd32).astype(BF16)
        inv = jnp.asarray(inv32).astype(BF16)
        spec = _filter_spectra(n, jnp.asarray(z), jnp.asarray(t), jnp.asarray(alt), fwd, *filt_args)
        hy = _hyena(hy_in, conv_w, conv_b, spec, fwd, inv, skip, nb, n)
        return _out_ffn(x2d, att, hy, m3, group_of_block, woa, woh, wg, wu, wd, *ln)

    xp = x_prompt.reshape(batch * seq, D_MODEL)
    ctx_group = lambda i: 0
    q, k, v, hy_in = _in_projection(xp, m3, w_in_b, ctx_group)
    y_prompt = mixer_and_ffn(xp, q, k, v, hy_in, ctx_group, batch, seq, None)
    new_cache_k = k.reshape(batch, 1, seq, 2 * DA_HEADS, DA_HEAD_DIM)
    new_cache_v = v.reshape(batch, 1, seq, DA_HEADS, 2 * DA_HEAD_DIM)

    xs = x_sample.reshape(dec_batch * dec_seq, D_MODEL)
    blocks_per_seq = dec_seq // ROW_BLOCK
    lat_group = lambda i: 1 + i // blocks_per_seq
    rope = tuple(jnp.asarray(a) for a in _rope_tables(dec_seq))
    q, k, v, hy_in = _in_projection(xs, m3, w_in_b, lat_group, rope)
    ctx = (cache_k[:, li].reshape(dec_batch, past, ATT_WIDTH),
           cache_v[:, li].reshape(dec_batch, past, ATT_WIDTH))
    y_sample = mixer_and_ffn(xs, q, k, v, hy_in, lat_group, dec_batch, dec_seq, ctx)

    return (y_prompt.reshape(batch, seq, D_MODEL),
            y_sample.reshape(dec_batch, dec_seq, D_MODEL),
            new_cache_k, new_cache_v)
```

```python
import functools
import math

import jax
import jax.numpy as jnp
import numpy as np
from jax import lax
from jax.experimental import pallas as pl
from jax.experimental.pallas import tpu as pltpu

D_MODEL = 1024
GRID_W = 64
ATT_WIDTH = D_MODEL // 2
HY_WIDTH = D_MODEL - ATT_WIDTH
DA_HEADS = 4
DA_HEAD_DIM = ATT_WIDTH // (2 * DA_HEADS)
PAIR_WIDTH = 2 * DA_HEAD_DIM
HY_EMB_DIM = 33
HY_BANDS = (HY_EMB_DIM - 1) // 2
HY_FILTER_WIDTH = 64
IN_COLS = 3 * ATT_WIDTH + 3 * HY_WIDTH
D_FF = 2816
DEPTH = 1
DEEPNORM_ALPHA = (2.0 * DEPTH) ** 0.25
ROPE_BASE = 10000.0
LN_EPS = 1e-5
SUBLANES = 8
MOD_ROWS = SUBLANES

ROW_BLOCK = 256
Q_ROWS = 256
HYENA_ROWS_PER_STEP = 512
VMEM_LIMIT = 56 * 1024 * 1024

F32 = jnp.float32
BF16 = jnp.bfloat16


def _params(*sem):
    return pltpu.CompilerParams(dimension_semantics=sem, vmem_limit_bytes=VMEM_LIMIT)


def _const_spec(shape):
    nd = len(shape)
    return pl.BlockSpec(shape, lambda *_: (0,) * nd, pipeline_mode=pl.Buffered(1))


def _ln(x):
    mu = jnp.mean(x, axis=-1, keepdims=True)
    xc = x - mu
    var = jnp.mean(xc * xc, axis=-1, keepdims=True)
    return xc * lax.rsqrt(var + LN_EPS)


def _dot(a, b):
    return jnp.dot(a, b, preferred_element_type=F32)


def _split(x):
    hi = x.astype(BF16)
    return hi, (x - hi.astype(F32)).astype(BF16)


def _dot3(a, b):
    a_hi, a_lo = _split(a)
    b_hi, b_lo = _split(b)
    return _dot(a_hi, b_hi) + (_dot(a_lo, b_hi) + _dot(a_hi, b_lo))


def _rope_tables(n):
    pos = np.arange(n)
    row = (pos // GRID_W).astype(np.float64)
    col = (pos % GRID_W).astype(np.float64)
    half = DA_HEAD_DIM // 2
    inv = ROPE_BASE ** (-np.arange(0, half, 2, dtype=np.float64) / half)
    ar = row[:, None] * inv
    ac = col[:, None] * inv
    cos = np.concatenate([np.cos(ar), np.cos(ar), np.cos(ac), np.cos(ac)], axis=1)
    sin = np.concatenate([-np.sin(ar), np.sin(ar), -np.sin(ac), np.sin(ac)], axis=1)
    reps = ATT_WIDTH // DA_HEAD_DIM
    return (np.tile(cos, (1, reps)).astype(np.float32),
            np.tile(sin, (1, reps)).astype(np.float32))


def _dft_tables(n):
    k = np.arange(n)[:, None]
    t = np.arange(n)[None, :]
    ang = np.pi * ((k * t) % (2 * n)).astype(np.float64) / n
    cosm = np.cos(ang)
    sinm = np.sin(ang)
    alt = np.where(np.arange(n) % 2 == 0, 1.0, -1.0)
    fwd_re = cosm
    fwd_im = -sinm
    fwd_im[0, :] = alt
    inv_re = cosm.T / n
    inv_re[:, 0] = 1.0 / (2 * n)
    inv_im = -sinm.T / n
    inv_im[:, 0] = alt / (2 * n)
    fwd = np.concatenate([fwd_re, fwd_im], axis=0).astype(np.float32)
    inv = np.concatenate([inv_re, inv_im], axis=1).astype(np.float32)
    return fwd, inv, alt.astype(np.float32)[:, None]


def _filter_features(n):
    pos = np.arange(n, dtype=np.float64)
    t = (pos / (n - 1))[:, None]
    bands = np.linspace(1e-4, HY_BANDS - 1, HY_BANDS)
    ang = (2.0 * math.pi / n) * pos[:, None] * bands
    z = np.concatenate([t, np.cos(ang), -np.sin(ang)], axis=-1)
    zp = np.zeros((n, HY_FILTER_WIDTH), np.float64)
    zp[:, :HY_EMB_DIM] = z
    return zp.astype(np.float32), t.astype(np.float32)


def _mod_kernel(c_ref, w_ref, b_ref, o_ref):
    c = c_ref[...]
    s = c * jax.nn.sigmoid(c)
    o_ref[...] = _dot3(s, w_ref[...]) + b_ref[...]


def _modulation(cvecs, mod_w, mod_b):
    tn = 1024
    return pl.pallas_call(
        _mod_kernel,
        grid=(6 * D_MODEL // tn,),
        in_specs=[pl.BlockSpec((MOD_ROWS, D_MODEL), lambda j: (0, 0)),
                  pl.BlockSpec((D_MODEL, tn), lambda j: (0, j)),
                  pl.BlockSpec((1, tn), lambda j: (0, j))],
        out_specs=pl.BlockSpec((MOD_ROWS, tn), lambda j: (0, j)),
        out_shape=jax.ShapeDtypeStruct((MOD_ROWS, 6 * D_MODEL), F32),
        compiler_params=_params("arbitrary"),
        name="modulation",
    )(cvecs, mod_w, mod_b)


def _rope(x, cos, sin):
    lane = lax.broadcasted_iota(jnp.int32, x.shape, 1) % (DA_HEAD_DIM // 2)
    width = x.shape[1]
    quarter = DA_HEAD_DIM // 4
    partner = jnp.where(lane < quarter,
                        pltpu.roll(x, width - quarter, 1),
                        pltpu.roll(x, quarter, 1))
    return x * cos + partner * sin


def _proj_kernel(*refs, latent):
    if latent:
        x_ref, m_ref, w_ref, cos_ref, sin_ref, q_ref, k_ref, v_ref, hy_ref = refs
    else:
        x_ref, m_ref, w_ref, q_ref, k_ref, v_ref, hy_ref, kc_ref, vc_ref = refs
    m = m_ref[0]
    sh1 = m[:, 0:D_MODEL]
    sc1 = m[:, D_MODEL:2 * D_MODEL]
    h = (_ln(x_ref[...]) * (1.0 + sc1) + sh1).astype(BF16)
    q = _dot(h, w_ref[:, 0:ATT_WIDTH])
    k = _dot(h, w_ref[:, ATT_WIDTH:2 * ATT_WIDTH])
    v = _dot(h, w_ref[:, 2 * ATT_WIDTH:3 * ATT_WIDTH])
    hy_ref[...] = _dot(h, w_ref[:, 3 * ATT_WIDTH:])
    if latent:
        cos = cos_ref[...]
        sin = sin_ref[...]
        q = _rope(q, cos, sin)
        k = _rope(k, cos, sin)
    q_ref[...] = (q * (DA_HEAD_DIM ** -0.5)).astype(q_ref.dtype)
    k_ref[...] = k.astype(k_ref.dtype)
    v_ref[...] = v.astype(v_ref.dtype)
    if not latent:
        kc_ref[...] = k.reshape(kc_ref.shape)
        vc_ref[...] = v.reshape(vc_ref.shape)


def _in_projection(x2d, m3, w_in, group_of_block, rope=None):
    tokens = x2d.shape[0]
    latent = rope is not None
    tm = ROW_BLOCK
    row = lambda i: (i, 0)
    in_specs = [pl.BlockSpec((tm, D_MODEL), row),
                pl.BlockSpec((1, 1, 6 * D_MODEL), lambda i: (group_of_block(i), 0, 0)),
                _const_spec((D_MODEL, IN_COLS))]
    args = [x2d, m3, w_in]
    if latent:
        n = rope[0].shape[0]
        blocks = n // tm
        tab = pl.BlockSpec((tm, ATT_WIDTH), lambda i: (i % blocks, 0))
        in_specs += [tab, tab]
        args += list(rope)
    out_specs = [pl.BlockSpec((tm, ATT_WIDTH), row),
                 pl.BlockSpec((tm, ATT_WIDTH), row),
                 pl.BlockSpec((tm, ATT_WIDTH), row),
                 pl.BlockSpec((tm, 3 * HY_WIDTH), row)]
    out_shape = [jax.ShapeDtypeStruct((tokens, ATT_WIDTH), BF16),
                 jax.ShapeDtypeStruct((tokens, ATT_WIDTH), BF16),
                 jax.ShapeDtypeStruct((tokens, ATT_WIDTH), BF16),
                 jax.ShapeDtypeStruct((tokens, 3 * HY_WIDTH), F32)]
    if not latent:
        row3 = lambda i: (i, 0, 0)
        out_specs += [pl.BlockSpec((tm, 2 * DA_HEADS, DA_HEAD_DIM), row3),
                      pl.BlockSpec((tm, DA_HEADS, 2 * DA_HEAD_DIM), row3)]
        out_shape += [jax.ShapeDtypeStruct((tokens, 2 * DA_HEADS, DA_HEAD_DIM), F32),
                      jax.ShapeDtypeStruct((tokens, DA_HEADS, 2 * DA_HEAD_DIM), F32)]
    return pl.pallas_call(
        functools.partial(_proj_kernel, latent=latent),
        grid=(tokens // tm,),
        in_specs=in_specs,
        out_specs=out_specs,
        out_shape=out_shape,
        compiler_params=_params("arbitrary"),
        name="in_projection_latent" if latent else "in_projection_context",
    )(*args)


def _attn_kernel(*refs, has_ctx, lam_init):
    if has_ctx:
        lq1, lk1, lq2, lk2, g_ref, q_ref, k_ref, v_ref, ck_ref, cv_ref, o_ref = refs
    else:
        lq1, lk1, lq2, lk2, g_ref, q_ref, k_ref, v_ref, o_ref = refs
    lam = (jnp.exp(jnp.sum(lq1[...] * lk1[...], axis=-1, keepdims=True))
           - jnp.exp(jnp.sum(lq2[...] * lk2[...], axis=-1, keepdims=True)) + lam_init)
    tq = q_ref.shape[0]
    lane = lax.broadcasted_iota(jnp.int32, (tq, PAIR_WIDTH), 1)
    nt = (((1,), (1,)), ((), ()))
    for h in range(DA_HEADS):
        cols = slice(h * PAIR_WIDTH, (h + 1) * PAIR_WIDTH)
        q = q_ref[:, cols]
        zero = jnp.zeros_like(q)
        qq = jnp.concatenate([jnp.where(lane < DA_HEAD_DIM, q, zero),
                              jnp.where(lane >= DA_HEAD_DIM, q, zero)], axis=0)
        keys = [k_ref[:, cols].astype(BF16)]
        vals = [v_ref[:, cols].astype(BF16)]
        if has_ctx:
            keys.append(ck_ref[0, :, cols].astype(BF16))
            vals.append(cv_ref[0, :, cols].astype(BF16))
        scores = [lax.dot_general(qq, kk, nt, preferred_element_type=F32) for kk in keys]
        mx = functools.reduce(jnp.maximum,
                              [jnp.max(s, axis=-1, keepdims=True) for s in scores])
        acc = None
        for s, vv in zip(scores, vals):
            e = jnp.exp(s - mx).astype(BF16)
            v_ext = jnp.concatenate([vv, jnp.ones_like(vv)], axis=1)
            part = _dot(e, v_ext)
            acc = part if acc is None else acc + part
        p = acc[:, :PAIR_WIDTH] * (1.0 / acc[:, PAIR_WIDTH:])
        o = p[:tq] - lam * p[tq:]
        ms = jnp.mean(o * o, axis=-1, keepdims=True)
        o_ref[:, cols] = (o * lax.rsqrt(ms + LN_EPS) * g_ref[...]
                          * (1.0 - lam_init)).astype(o_ref.dtype)


def _attention(q, k, v, lam_vecs, subln, batch, n, ctx=None):
    has_ctx = ctx is not None
    tq = Q_ROWS
    qb = n // tq
    small = pl.BlockSpec((1, DA_HEAD_DIM), lambda b, i: (0, 0))
    in_specs = [small, small, small, small,
                pl.BlockSpec((1, PAIR_WIDTH), lambda b, i: (0, 0)),
                pl.BlockSpec((tq, ATT_WIDTH), lambda b, i: (b * qb + i, 0)),
                pl.BlockSpec((n, ATT_WIDTH), lambda b, i: (b, 0)),
                pl.BlockSpec((n, ATT_WIDTH), lambda b, i: (b, 0))]
    args = list(lam_vecs) + [subln, q, k, v]
    if has_ctx:
        m = ctx[0].shape[1]
        cspec = pl.BlockSpec((1, m, ATT_WIDTH), lambda b, i: (b, 0, 0))
        in_specs += [cspec, cspec]
        args += list(ctx)
    lam_init = 0.8 - 0.6 * math.exp(-0.3 * 0)
    return pl.pallas_call(
        functools.partial(_attn_kernel, has_ctx=has_ctx, lam_init=lam_init),
        grid=(batch, qb),
        in_specs=in_specs,
        out_specs=pl.BlockSpec((tq, ATT_WIDTH), lambda b, i: (b * qb + i, 0)),
        out_shape=jax.ShapeDtypeStruct((batch * n, ATT_WIDTH), BF16),
        compiler_params=_params("arbitrary", "arbitrary"),
        name="diff_attention_latent" if has_ctx else "diff_attention_context",
    )(*args)


def _filter_kernel(z_ref, t_ref, alt_ref, w1_ref, b1_ref, w2_ref, b2_ref, fr_ref, w3_ref,
                   dec_ref, fwd_ref, o_ref, hdn_ref):
    n = z_ref.shape[0]

    @pl.when(pl.program_id(0) == 0)
    def _():
        fr = fr_ref[...]
        hdn = jnp.sin(fr * (_dot3(z_ref[...], w1_ref[...]) + b1_ref[...]))
        hdn_ref[...] = jnp.sin(fr * (_dot3(hdn, w2_ref[...]) + b2_ref[...]))

    filt = _dot3(hdn_ref[...], w3_ref[...])
    filt = filt * jnp.exp(-t_ref[...] * jnp.abs(dec_ref[...]))
    h_fwd = filt[:, :HY_WIDTH]
    row = lax.broadcasted_iota(jnp.int32, (n, HY_WIDTH), 0)
    h_bwd = jnp.where(row == 0, 0.0, filt[:, HY_WIDTH:])
    h_sum = h_fwd + h_bwd
    h_dif = h_fwd - h_bwd
    spec_re = _dot(fwd_ref[0:n, :], h_sum.astype(BF16))
    spec_im = _dot(fwd_ref[n:2 * n, :], h_dif.astype(BF16))
    nyquist = jnp.sum(h_sum * alt_ref[...], axis=0, keepdims=True)
    o_ref[0] = spec_re
    o_ref[1] = jnp.where(row == 0, 0.0, spec_im)
    o_ref[2] = jnp.where(row == 0, nyquist, spec_re)


def _filter_spectra(n, z, t, alt, fwd, w1p, b1, w2, b2, fr, w3, dec):
    fw = HY_FILTER_WIDTH
    cols = 2 * HY_WIDTH
    c2 = lambda o: (0, 0)
    return pl.pallas_call(
        _filter_kernel,
        grid=(2,),
        in_specs=[pl.BlockSpec((n, fw), c2), pl.BlockSpec((n, 1), c2), pl.BlockSpec((n, 1), c2),
                  pl.BlockSpec((fw, fw), c2), pl.BlockSpec((1, fw), c2),
                  pl.BlockSpec((fw, fw), c2), pl.BlockSpec((1, fw), c2),
                  pl.BlockSpec((1, fw), c2),
                  pl.BlockSpec((fw, cols), lambda o: (0, o)),
                  pl.BlockSpec((1, cols), lambda o: (0, o)),
                  _const_spec((2 * n, n))],
        out_specs=pl.BlockSpec((3, n, HY_WIDTH), lambda o: (o, 0, 0)),
        out_shape=jax.ShapeDtypeStruct((6, n, HY_WIDTH), F32),
        scratch_shapes=[pltpu.VMEM((n, fw), F32)],
        compiler_params=_params("arbitrary"),
        name=f"hyena_filter_spectra_{n}",
    )(z, t, alt, w1p, b1, w2, b2, fr, w3, dec, fwd)


def _hyena_kernel(u_ref, cw_ref, cb_ref, spec_ref, fwd_ref, inv_ref, skip_ref, o_ref, *, seqs):
    n = u_ref.shape[0] // seqs

    def long_conv(sig, order):
        sb = sig.astype(BF16)
        u_re = _dot(fwd_ref[0:n, :], sb)
        u_im = _dot(fwd_ref[n:2 * n, :], sb)
        a = spec_ref[3 * order]
        b = spec_ref[3 * order + 1]
        d = spec_ref[3 * order + 2]
        y_re = (u_re * a - u_im * b).astype(BF16)
        y_im = (u_re * b + u_im * d).astype(BF16)
        y = _dot(inv_ref[:, 0:n], y_re) + _dot(inv_ref[:, n:2 * n], y_im)
        return y + sig * skip_ref[order:order + 1, :]

    for s in range(seqs):
        rows = slice(s * n, (s + 1) * n)
        u = u_ref[rows, :]
        row = lax.broadcasted_iota(jnp.int32, u.shape, 0)
        u_prev = jnp.where(row == 0, 0.0, pltpu.roll(u, 1, 0))
        u_next = jnp.where(row == n - 1, 0.0, pltpu.roll(u, n - 1, 0))
        uc = (u_prev * cw_ref[0:1, :] + u * cw_ref[1:2, :] + u_next * cw_ref[2:3, :]
              + cb_ref[...])
        v = uc[:, 0:HY_WIDTH]
        x1 = uc[:, HY_WIDTH:2 * HY_WIDTH]
        x2 = uc[:, 2 * HY_WIDTH:]
        z = x1 * long_conv(v, 0)
        o_ref[rows, :] = (x2 * long_conv(z, 1)).astype(o_ref.dtype)


def _hyena(hy_in, conv_w, conv_b, spec, fwd, inv, skip, batch, n):
    c2 = lambda b: (0, 0)
    seqs = max(1, min(batch, HYENA_ROWS_PER_STEP // n))
    in_specs = [pl.BlockSpec((seqs * n, 3 * HY_WIDTH), lambda b: (b, 0)),
                pl.BlockSpec((3, 3 * HY_WIDTH), c2),
                pl.BlockSpec((1, 3 * HY_WIDTH), c2),
                _const_spec((6, n, HY_WIDTH)),
                _const_spec((2 * n, n)),
                _const_spec((n, 2 * n)),
                pl.BlockSpec((2, HY_WIDTH), c2)]
    args = [hy_in, conv_w, conv_b, spec, fwd, inv, skip]
    return pl.pallas_call(
        functools.partial(_hyena_kernel, seqs=seqs),
        grid=(batch // seqs,),
        in_specs=in_specs,
        out_specs=pl.BlockSpec((seqs * n, HY_WIDTH), lambda b: (b, 0)),
        out_shape=jax.ShapeDtypeStruct((batch * n, HY_WIDTH), BF16),
        compiler_params=_params("arbitrary"),
        name=f"hyena_mixer_{n}",
    )(*args)


def _ffn_kernel(x_ref, att_ref, hy_ref, m_ref, woa_ref, woh_ref, wg_ref, wu_ref, wd_ref,
                g1_ref, b1_ref, g2_ref, b2_ref, o_ref):
    m = m_ref[0]
    gate1 = m[:, 2 * D_MODEL:3 * D_MODEL]
    sh2 = m[:, 3 * D_MODEL:4 * D_MODEL]
    sc2 = m[:, 4 * D_MODEL:5 * D_MODEL]
    gate2 = m[:, 5 * D_MODEL:6 * D_MODEL]
    mix = _dot(att_ref[...], woa_ref[...]) + _dot(hy_ref[...], woh_ref[...])
    x1 = _ln(DEEPNORM_ALPHA * x_ref[...] + gate1 * mix) * g1_ref[...] + b1_ref[...]
    h = (_ln(x1) * (1.0 + sc2) + sh2).astype(BF16)
    g = _dot(h, wg_ref[...])
    up = _dot(h, wu_ref[...])
    act = (g * jax.nn.sigmoid(g) * up).astype(BF16)
    ffn = _dot(act, wd_ref[...])
    o_ref[...] = _ln(DEEPNORM_ALPHA * x1 + gate2 * ffn) * g2_ref[...] + b2_ref[...]


def _out_ffn(x2d, att, hy, m3, group_of_block, woa, woh, wg, wu, wd, ln1g, ln1b, ln2g, ln2b):
    tokens = x2d.shape[0]
    tm = ROW_BLOCK
    row = lambda i: (i, 0)
    vec = pl.BlockSpec((1, D_MODEL), lambda i: (0, 0))
    return pl.pallas_call(
        _ffn_kernel,
        grid=(tokens // tm,),
        in_specs=[pl.BlockSpec((tm, D_MODEL), row),
                  pl.BlockSpec((tm, ATT_WIDTH), row),
                  pl.BlockSpec((tm, HY_WIDTH), row),
                  pl.BlockSpec((1, 1, 6 * D_MODEL), lambda i: (group_of_block(i), 0, 0)),
                  _const_spec((ATT_WIDTH, D_MODEL)),
                  _const_spec((HY_WIDTH, D_MODEL)),
                  _const_spec((D_MODEL, D_FF)),
                  _const_spec((D_MODEL, D_FF)),
                  _const_spec((D_FF, D_MODEL)),
                  vec, vec, vec, vec],
        out_specs=pl.BlockSpec((tm, D_MODEL), row),
        out_shape=jax.ShapeDtypeStruct((tokens, D_MODEL), F32),
        compiler_params=_params("arbitrary"),
        name="out_proj_ffn",
    )(x2d, att, hy, m3, woa, woh, wg, wu, wd, ln1g, ln1b, ln2g, ln2b)


def kernel(x_prompt, x_sample, cache_k, cache_v, c, c_ctx, mod_w, mod_b, w_in, da_lq1, da_lk1, da_lq2, da_lk2, da_subln, hy_conv_w, hy_conv_b, hy_w1, hy_b1, hy_w2, hy_b2, hy_freq, hy_w3, hy_decay, hy_skip, w_out, ln1_g, ln1_b, w_up, w_down, ln2_g, ln2_b):
    li = 0
    batch, seq, _ = x_prompt.shape
    dec_batch, dec_seq, _ = x_sample.shape
    past = cache_k.shape[2]

    cvecs = jnp.concatenate(
        [c_ctx[None, :], c, jnp.zeros((MOD_ROWS - 1 - dec_batch, D_MODEL), F32)], axis=0)
    m3 = _modulation(cvecs, mod_w[li], mod_b[li][None, :]).reshape(MOD_ROWS, 1, 6 * D_MODEL)

    w_in_b = w_in[li].astype(BF16)
    woa = w_out[li, :ATT_WIDTH].astype(BF16)
    woh = w_out[li, ATT_WIDTH:].astype(BF16)
    wg = w_up[li, :, :D_FF].astype(BF16)
    wu = w_up[li, :, D_FF:].astype(BF16)
    wd = w_down[li].astype(BF16)
    lam_vecs = [a[li][None, :] for a in (da_lq1, da_lk1, da_lq2, da_lk2)]
    subln = da_subln[li][None, :]
    ln = [a[li][None, :] for a in (ln1_g, ln1_b, ln2_g, ln2_b)]

    w1p = jnp.zeros((HY_FILTER_WIDTH, HY_FILTER_WIDTH), F32).at[:HY_EMB_DIM].set(hy_w1[li])
    filt_args = (w1p, hy_b1[li][None, :], hy_w2[li], hy_b2[li][None, :], hy_freq[li][None, :],
                 hy_w3[li], hy_decay[li].reshape(1, -1))
    conv_w = hy_conv_w[li]
    conv_b = hy_conv_b[li][None, :]
    skip = hy_skip[li]

    def mixer_and_ffn(x2d, q, k, v, hy_in, group_of_block, nb, n, ctx):
        att = _attention(q, k, v, lam_vecs, subln, nb, n, ctx)
        fwd32, inv32, alt = _dft_tables(n)
        z, t = _filter_features(n)
        fwd = jnp.asarray(fwd32).astype(BF16)
        inv = jnp.asarray(inv32).astype(BF16)
        spec = _filter_spectra(n, jnp.asarray(z), jnp.asarray(t), jnp.asarray(alt), fwd, *filt_args)
        hy = _hyena(hy_in, conv_w, conv_b, spec, fwd, inv, skip, nb, n)
        return _out_ffn(x2d, att, hy, m3, group_of_block, woa, woh, wg, wu, wd, *ln)

    xp = x_prompt.reshape(batch * seq, D_MODEL)
    ctx_group = lambda i: 0
    q, k, v, hy_in, k_cache, v_cache = _in_projection(xp, m3, w_in_b, ctx_group)
    y_prompt = mixer_and_ffn(xp, q, k, v, hy_in, ctx_group, batch, seq, None)
    new_cache_k = k_cache.reshape(batch, 1, seq, 2 * DA_HEADS, DA_HEAD_DIM)
    new_cache_v = v_cache.reshape(batch, 1, seq, DA_HEADS, 2 * DA_HEAD_DIM)

    xs = x_sample.reshape(dec_batch * dec_seq, D_MODEL)
    blocks_per_seq = dec_seq // ROW_BLOCK
    lat_group = lambda i: 1 + i // blocks_per_seq
    rope = tuple(jnp.asarray(a) for a in _rope_tables(dec_seq))
    q, k, v, hy_in = _in_projection(xs, m3, w_in_b, lat_group, rope)
    ctx = (cache_k[:, li].reshape(dec_batch, past, ATT_WIDTH),
           cache_v[:, li].reshape(dec_batch, past, ATT_WIDTH))
    y_sample = mixer_and_ffn(xs, q, k, v, hy_in, lat_group, dec_batch, dec_seq, ctx)

    return (y_prompt.reshape(batch, seq, D_MODEL),
            y_sample.reshape(dec_batch, dec_seq, D_MODEL),
            new_cache_k, new_cache_v)
```
